```python
import math
import jax, jax.numpy as jnp
from jax import lax
import numpy as np

D_MODEL = 2048
BATCH = 8
SEQ = 4096
DEPTH = 1

CTX_LEN = 256
GRID_W = 64
POS_BASE = 10000.0
LN_EPS = 1e-5

SSD_D_INNER = 2048
SSD_HEAD_DIM = 64
SSD_HEADS = SSD_D_INNER // SSD_HEAD_DIM
SSD_GROUPS = 4
SSD_STATE = 128
SSD_CHUNK = 128
CONV_W = 5
XBC_DIM = SSD_D_INNER + 2 * SSD_GROUPS * SSD_STATE
SSD_COLS = XBC_DIM + 2 * SSD_HEADS

CMLP_WIDTH = 2048
CMLP_GROUPS = 8
CMLP_GROUP_DIM = CMLP_WIDTH // CMLP_GROUPS
CMLP_CHUNK = 128

IN_DIM = SSD_COLS + SSD_D_INNER + 2 * CMLP_WIDTH + 2 * D_MODEL

N_EXPERTS = 64
EXPERT_DIM = 512
SHARED_DIM = 512
TOP_K = 8
N_EXPERT_GROUPS = 8
TOPK_GROUPS = 4
ROUTED_SCALE = 2.5

DEEPNORM_ALPHA = (2.0 * DEPTH) ** 0.25
DEEPNORM_BETA = (8.0 * DEPTH) ** -0.25

kernel_name = "hybrid_ssd_chunkmlp_moe_prefix_dit"


def layer_norm(h, g, b):
    hf = h.astype(jnp.float32)
    mu = jnp.mean(hf, -1, keepdims=True)
    var = jnp.mean(jnp.square(hf - mu), -1, keepdims=True)
    return ((hf - mu) * lax.rsqrt(var + LN_EPS) * g + b).astype(h.dtype)


def rms_norm(h, g):
    hf = h.astype(jnp.float32)
    return hf * lax.rsqrt(jnp.mean(jnp.square(hf), -1, keepdims=True) + LN_EPS) * g


def modulate(h, shift, scale):
    return h * (1 + scale) + shift


def adaln(cvec, w, b):
    return (jax.nn.silu(cvec) @ w + b)[:, None, :]


def sincos_2d(rows, dim):
    quarter = dim // 4
    omega = 1.0 / POS_BASE ** (jnp.arange(quarter, dtype=jnp.float32) / quarter)
    r = jnp.broadcast_to(jnp.arange(rows, dtype=jnp.float32)[:, None], (rows, GRID_W)).reshape(-1)
    col = jnp.broadcast_to(jnp.arange(GRID_W, dtype=jnp.float32)[None, :], (rows, GRID_W)).reshape(-1)
    ar = r[:, None] * omega
    ac = col[:, None] * omega
    return jnp.concatenate([jnp.sin(ar), jnp.cos(ar), jnp.sin(ac), jnp.cos(ac)], -1)


def dwconv_centred(u, w, b):
    y = lax.conv_general_dilated(u, w[:, None, :].astype(u.dtype), window_strides=(1,),
                                 padding=[(CONV_W // 2, CONV_W // 2)],
                                 dimension_numbers=('NWC', 'WIO', 'NWC'),
                                 feature_group_count=u.shape[-1])
    return y + b


def segsum(a):
    n = a.shape[-1]
    cs = jnp.cumsum(a, axis=-1)
    diff = cs[..., :, None] - cs[..., None, :]
    return jnp.where(jnp.tril(jnp.ones((n, n), dtype=bool)), diff, -jnp.inf)


def ssd_scan(xh, dt, A, Bm, Cm, init_state, with_output):
    f32 = jnp.float32
    bsz, L, H, P = xh.shape
    G, N = Bm.shape[-2], Bm.shape[-1]
    R = H // G
    Q = SSD_CHUNK
    nc = L // Q
    x = (xh.astype(f32) * dt[..., None]).reshape(bsz, nc, Q, G, R, P)
    a = jnp.moveaxis((dt * A).reshape(bsz, nc, Q, G, R), 2, -1)
    a_cs = jnp.cumsum(a, axis=-1)
    Bc = Bm.astype(f32).reshape(bsz, nc, Q, G, N)
    decay_to_end = jnp.moveaxis(jnp.exp(a_cs[..., -1:] - a_cs), -1, 2)[..., None]
    local_states = jnp.einsum('bclgn,bclgrp->bcgrpn', Bc, x * decay_to_end)
    init = init_state.astype(f32).reshape(bsz, 1, G, R, P, N)
    states = jnp.concatenate([init, local_states], axis=1)
    chunk_a = jnp.pad(jnp.moveaxis(a_cs[..., -1], 1, -1), ((0, 0), (0, 0), (0, 0), (1, 0)))
    carried = jnp.einsum('bgrzc,bcgrpn->bzgrpn', jnp.exp(segsum(chunk_a)), states)
    final = carried[:, -1].reshape(bsz, H, P, N)
    if not with_output:
        return None, final
    Cc = Cm.astype(f32).reshape(bsz, nc, Q, G, N)
    scores = jnp.einsum('bclgn,bcsgn->bcgls', Cc, Bc)
    y_diag = jnp.einsum('bcgrls,bcsgrp->bclgrp', scores[:, :, :, None] * jnp.exp(segsum(a)), x)
    decay_from_start = jnp.moveaxis(jnp.exp(a_cs), -1, 2)[..., None]
    y_off = jnp.einsum('bclgn,bcgrpn->bclgrp', Cc, carried[:, :-1]) * decay_from_start
    return (y_diag + y_off).reshape(bsz, L, H, P), final


def ssd_prep(p_ssd, conv_w, conv_b, dt_bias):
    bsz, L, _ = p_ssd.shape
    xbc = jax.nn.silu(dwconv_centred(p_ssd[..., :XBC_DIM], conv_w, conv_b))
    gn = SSD_GROUPS * SSD_STATE
    xh = xbc[..., :SSD_D_INNER].reshape(bsz, L, SSD_HEADS, SSD_HEAD_DIM)
    Bm = xbc[..., SSD_D_INNER:SSD_D_INNER + gn].reshape(bsz, L, SSD_GROUPS, SSD_STATE)
    Cm = xbc[..., SSD_D_INNER + gn:].reshape(bsz, L, SSD_GROUPS, SSD_STATE)
    dt = jax.nn.softplus(p_ssd[..., XBC_DIM:].astype(jnp.float32).reshape(bsz, L, 2, SSD_HEADS)
                         + dt_bias.astype(jnp.float32))
    return xh, Bm, Cm, dt


def ssd_bidir(xh, Bm, Cm, dt, a_log, init_f, init_b, with_output):
    A = -jnp.exp(a_log.astype(jnp.float32))
    flip = lambda t: jnp.flip(t, axis=1)
    y_f, s_f = ssd_scan(xh, dt[:, :, 0], A[0], Bm, Cm, init_f, with_output)
    y_b, s_b = ssd_scan(flip(xh), flip(dt[:, :, 1]), A[1], flip(Bm), flip(Cm), init_b, with_output)
    y = y_f + flip(y_b) if with_output else None
    return y, s_f, s_b


def chunk_mlp(u, v, ln_g, ln_b, w_s, b_s):
    bsz, L, _ = v.shape
    v = layer_norm(jax.nn.gelu(v), ln_g, ln_b)
    vc = v.reshape(bsz, L // CMLP_CHUNK, CMLP_CHUNK, CMLP_GROUPS, CMLP_GROUP_DIM)
    mixed = jnp.einsum('gts,bnsgd->bntgd', w_s, vc) + b_s.T[None, None, :, :, None]
    return jax.nn.gelu(u) * mixed.reshape(bsz, L, CMLP_WIDTH)


def token_mixer(a, init_f, init_b, lp):
    bsz, L, _ = a.shape
    proj = a @ lp['w_in']
    xh, Bm, Cm, dt = ssd_prep(proj[..., :SSD_COLS], lp['conv_w'], lp['conv_b'], lp['dt_bias'])
    o = SSD_COLS
    z = proj[..., o:o + SSD_D_INNER]
    o += SSD_D_INNER
    u = proj[..., o:o + CMLP_WIDTH]
    o += CMLP_WIDTH
    v = proj[..., o:o + CMLP_WIDTH]
    o += CMLP_WIDTH
    gate_ssd = proj[..., o:o + D_MODEL]
    gate_cm = proj[..., o + D_MODEL:]
    y, s_f, s_b = ssd_bidir(xh, Bm, Cm, dt, lp['a_log'], init_f, init_b, True)
    y = (y + lp['d_skip'].astype(jnp.float32)[:, None] * xh.astype(jnp.float32)).reshape(bsz, L, SSD_D_INNER)
    y = rms_norm(y * jax.nn.silu(z.astype(jnp.float32)), lp['ssd_norm_w']).astype(a.dtype)
    y_ssd = y @ lp['w_ssd_br']
    y_cm = chunk_mlp(u, v, lp['cmlp_ln_g'], lp['cmlp_ln_b'], lp['cmlp_ws'], lp['cmlp_bs']) @ lp['w_cmlp_br']
    merged = jax.nn.sigmoid(gate_ssd) * y_ssd + jax.nn.sigmoid(gate_cm) * y_cm
    return merged @ lp['w_o'], s_f, s_b


def context_ssd_states(a, lp):
    bsz = a.shape[0]
    proj = a @ lp['w_in'][:, :SSD_COLS]
    xh, Bm, Cm, dt = ssd_prep(proj, lp['conv_w'], lp['conv_b'], lp['dt_bias'])
    zero = jnp.zeros((bsz, SSD_HEADS, SSD_HEAD_DIM, SSD_STATE), jnp.float32)
    _, s_f, s_b = ssd_bidir(xh, Bm, Cm, dt, lp['a_log'], zero, zero, False)
    return s_f, s_b


def moe_ffn(h, w_router, router_bias, w_e_gate, w_e_up, w_e_down, w_sh_gate, w_sh_up, w_sh_down):
    shape = h.shape
    t = h.reshape(-1, shape[-1])
    n_tok = t.shape[0]
    scores = jax.nn.sigmoid(t.astype(jnp.float32) @ w_router.astype(jnp.float32))
    biased = scores + router_bias.astype(jnp.float32)
    per_group = N_EXPERTS // N_EXPERT_GROUPS
    group_score = lax.top_k(biased.reshape(n_tok, N_EXPERT_GROUPS, per_group), 2)[0].sum(-1)
    _, top_groups = lax.top_k(group_score, TOPK_GROUPS)
    group_mask = jax.nn.one_hot(top_groups, N_EXPERT_GROUPS, dtype=jnp.float32).sum(1) > 0
    expert_mask = jnp.repeat(group_mask, per_group, axis=1)
    _, top_experts = lax.top_k(jnp.where(expert_mask, biased, -jnp.inf), TOP_K)
    w = jnp.take_along_axis(scores, top_experts, axis=1)
    w = ROUTED_SCALE * w / jnp.sum(w, -1, keepdims=True)
    gates = jnp.zeros((n_tok, N_EXPERTS), jnp.float32).at[jnp.arange(n_tok)[:, None], top_experts].set(w)

    def add_expert(acc, params):
        wg, wu, wd, g = params
        hid = jax.nn.silu(t @ wg) * (t @ wu)
        return acc + g[:, None] * (hid @ wd).astype(jnp.float32), None

    routed, _ = lax.scan(add_expert, jnp.zeros((n_tok, shape[-1]), jnp.float32),
                         (w_e_gate, w_e_up, w_e_down, gates.T))
    shared = (jax.nn.silu(t @ w_sh_gate) * (t @ w_sh_up)) @ w_sh_down
    return (routed + shared.astype(jnp.float32)).astype(h.dtype).reshape(shape)


def setup_inputs(seed: int = 0) -> dict:
    key = jax.random.key(seed)
    ks = jax.random.split(key, 40)
    f32 = jnp.float32
    L = DEPTH

    def nrm(k, shape, s):
        return jax.random.normal(k, shape, f32) * s

    dt0 = jnp.exp(jax.random.uniform(ks[10], (L, 2, SSD_HEADS), f32, math.log(1e-3), math.log(1e-1)))
    return {
        'x': nrm(ks[0], (BATCH, SEQ, D_MODEL), 1.0),
        'c': nrm(ks[1], (BATCH, D_MODEL), 1.0),
        'ctx': nrm(ks[2], (BATCH, CTX_LEN, D_MODEL), 1.0),
        'c_ctx': nrm(ks[3], (D_MODEL,), 1.0),
        'ln_in_g': 1.0 + nrm(ks[4], (D_MODEL,), 0.02),
        'ln_in_b': nrm(ks[5], (D_MODEL,), 0.02),
        'w_ada': nrm(ks[6], (L, D_MODEL, 6 * D_MODEL), 0.5 * D_MODEL ** -0.5),
        'b_ada': nrm(ks[7], (L, 6 * D_MODEL), 0.02),
        'w_in': nrm(ks[8], (L, D_MODEL, IN_DIM), D_MODEL ** -0.5),
        'conv_w': nrm(ks[9], (L, CONV_W, XBC_DIM), CONV_W ** -0.5),
        'conv_b': nrm(ks[11], (L, XBC_DIM), 0.02),
        'dt_bias': dt0 + jnp.log(-jnp.expm1(-dt0)),
        'a_log': jnp.log(jax.random.uniform(ks[12], (L, 2, SSD_HEADS), f32, 1.0, 16.0)),
        'd_skip': 1.0 + nrm(ks[13], (L, SSD_HEADS), 0.1),
        'ssd_norm_w': 1.0 + nrm(ks[14], (L, SSD_D_INNER), 0.02),
        'w_ssd_br': nrm(ks[15], (L, SSD_D_INNER, D_MODEL), SSD_D_INNER ** -0.5),
        'cmlp_ln_g': 1.0 + nrm(ks[16], (L, CMLP_WIDTH), 0.02),
        'cmlp_ln_b': nrm(ks[17], (L, CMLP_WIDTH), 0.02),
        'cmlp_ws': nrm(ks[18], (L, CMLP_GROUPS, CMLP_CHUNK, CMLP_CHUNK), 0.5 * CMLP_CHUNK ** -0.5),
        'cmlp_bs': 1.0 + nrm(ks[19], (L, CMLP_GROUPS, CMLP_CHUNK), 0.1),
        'w_cmlp_br': nrm(ks[20], (L, CMLP_WIDTH, D_MODEL), CMLP_WIDTH ** -0.5),
        'w_o': nrm(ks[21], (L, D_MODEL, D_MODEL), DEEPNORM_BETA * D_MODEL ** -0.5),
        'ln1_g': 1.0 + nrm(ks[22], (L, D_MODEL), 0.02),
        'ln1_b': nrm(ks[23], (L, D_MODEL), 0.02),
        'w_router': nrm(ks[24], (L, D_MODEL, N_EXPERTS), D_MODEL ** -0.5),
        'router_bias': nrm(ks[25], (L, N_EXPERTS), 0.01),
        'w_e_gate': nrm(ks[26], (L, N_EXPERTS, D_MODEL, EXPERT_DIM), D_MODEL ** -0.5),
        'w_e_up': nrm(ks[27], (L, N_EXPERTS, D_MODEL, EXPERT_DIM), D_MODEL ** -0.5),
        'w_e_down': nrm(ks[28], (L, N_EXPERTS, EXPERT_DIM, D_MODEL), DEEPNORM_BETA * EXPERT_DIM ** -0.5),
        'w_sh_gate': nrm(ks[29], (L, D_MODEL, SHARED_DIM), D_MODEL ** -0.5),
        'w_sh_up': nrm(ks[30], (L, D_MODEL, SHARED_DIM), D_MODEL ** -0.5),
        'w_sh_down': nrm(ks[31], (L, SHARED_DIM, D_MODEL), DEEPNORM_BETA * SHARED_DIM ** -0.5),
        'ln2_g': 1.0 + nrm(ks[32], (L, D_MODEL), 0.02),
        'ln2_b': nrm(ks[33], (L, D_MODEL), 0.02),
    }


def reference(x, c, ctx, c_ctx, ln_in_g, ln_in_b, w_ada, b_ada, w_in, conv_w, conv_b, dt_bias, a_log,
              d_skip, ssd_norm_w, w_ssd_br, cmlp_ln_g, cmlp_ln_b, cmlp_ws, cmlp_bs, w_cmlp_br, w_o,
              ln1_g, ln1_b, w_router, router_bias, w_e_gate, w_e_up, w_e_down, w_sh_gate, w_sh_up,
              w_sh_down, ln2_g, ln2_b):
    bsz, n_lat, _ = x.shape
    rows = n_lat // GRID_W
    pos = sincos_2d(rows, D_MODEL).astype(x.dtype)
    h_lat = layer_norm(x + pos, ln_in_g, ln_in_b)
    h_ctx = layer_norm(ctx, ln_in_g, ln_in_b)
    zero_state = jnp.zeros((bsz, SSD_HEADS, SSD_HEAD_DIM, SSD_STATE), jnp.float32)
    for i in range(DEPTH):
        lp = {'w_in': w_in[i], 'conv_w': conv_w[i], 'conv_b': conv_b[i], 'dt_bias': dt_bias[i],
              'a_log': a_log[i], 'd_skip': d_skip[i], 'ssd_norm_w': ssd_norm_w[i], 'w_ssd_br': w_ssd_br[i],
              'cmlp_ln_g': cmlp_ln_g[i], 'cmlp_ln_b': cmlp_ln_b[i], 'cmlp_ws': cmlp_ws[i],
              'cmlp_bs': cmlp_bs[i], 'w_cmlp_br': w_cmlp_br[i], 'w_o': w_o[i]}
        moe_p = (w_router[i], router_bias[i], w_e_gate[i], w_e_up[i], w_e_down[i],
                 w_sh_gate[i], w_sh_up[i], w_sh_down[i])
        last = i == DEPTH - 1
        sh1, sc1, g1, sh2, sc2, g2 = jnp.split(adaln(c, w_ada[i], b_ada[i]), 6, axis=-1)
        csh1, csc1, cg1, csh2, csc2, cg2 = jnp.split(adaln(c_ctx[None], w_ada[i], b_ada[i]), 6, axis=-1)
        a_ctx = modulate(h_ctx, csh1, csc1)
        if last:
            s_f, s_b = context_ssd_states(a_ctx, lp)
        else:
            m_ctx, s_f, s_b = token_mixer(a_ctx, zero_state, zero_state, lp)
        m_lat, _, _ = token_mixer(modulate(h_lat, sh1, sc1), s_f, s_b, lp)
        h_lat = layer_norm(DEEPNORM_ALPHA * h_lat + g1 * m_lat, ln1_g[i], ln1_b[i])
        a2_lat = modulate(h_lat, sh2, sc2)
        if last:
            f_lat = moe_ffn(a2_lat, *moe_p)
        else:
            h_ctx = layer_norm(DEEPNORM_ALPHA * h_ctx + cg1 * m_ctx, ln1_g[i], ln1_b[i])
            a2_ctx = modulate(h_ctx, csh2, csc2)
            f_all = moe_ffn(jnp.concatenate([a2_lat, a2_ctx], axis=1), *moe_p)
            f_lat, f_ctx = f_all[:, :n_lat], f_all[:, n_lat:]
            h_ctx = layer_norm(DEEPNORM_ALPHA * h_ctx + cg2 * f_ctx, ln2_g[i], ln2_b[i])
        h_lat = layer_norm(DEEPNORM_ALPHA * h_lat + g2 * f_lat, ln2_g[i], ln2_b[i])
    return h_lat
```

```python
import functools
import math

import jax
import jax.numpy as jnp
from jax import lax
from jax.experimental import pallas as pl
from jax.experimental.pallas import tpu as pltpu

F32 = jnp.float32
BF16 = jnp.bfloat16
I32 = jnp.int32
U32 = jnp.uint32

GRID_W = 64
POS_BASE = 10000.0
LN_EPS = 1e-5
SSD_HEAD_DIM = 64
SSD_GROUPS = 4
SSD_STATE = 128
SSD_CHUNK = 128
CONV_W = 5
CMLP_GROUPS = 8
CMLP_CHUNK = 128
TOP_K = 8
N_EXPERT_GROUPS = 8
TOPK_GROUPS = 4
ROUTED_SCALE = 2.5
DEPTH = 1
DEEPNORM_ALPHA = (2.0 * DEPTH) ** 0.25

LANES = 128
SUBLANES = 8
BF16_ROWS = 16
VMEM_LIMIT_BYTES = 56 * 1024 * 1024

MOE_TILE_ROWS = 256
HI_MASK = 0xFFFF0000


def _tile(n, pref, align):
    if n <= pref:
        return n
    t = (pref // align) * align
    while t >= align:
        if n % t == 0:
            return t
        t -= align
    raise ValueError(f"no {align}-aligned tile of {n} below {pref}")


def _params(*sem, vmem=None):
    return pltpu.CompilerParams(dimension_semantics=sem, vmem_limit_bytes=vmem)


def _ln(x, g, b):
    mu = jnp.mean(x, axis=-1, keepdims=True)
    xc = x - mu
    var = jnp.mean(xc * xc, axis=-1, keepdims=True)
    return xc * lax.rsqrt(var + LN_EPS) * g + b


def _silu(x):
    return x * jax.nn.sigmoid(x)


def _gelu_tanh(x):
    return 0.5 * x * (1.0 + jnp.tanh(math.sqrt(2.0 / math.pi) * (x + 0.044715 * (x * x * x))))


def _softplus(x):
    return jnp.maximum(x, 0.0) + jnp.log1p(jnp.exp(-jnp.abs(x)))


def _split3(v):
    h1 = v.astype(BF16)
    r1 = v - h1.astype(F32)
    h2 = r1.astype(BF16)
    h3 = (r1 - h2.astype(F32)).astype(BF16)
    return h1, h2, h3


def _pack_halves(v):
    m = v.shape[1] // 2
    lo = lax.bitcast_convert_type(v[:, :m].astype(BF16).astype(F32), U32)
    hi = lax.bitcast_convert_type(v[:, m:].astype(BF16).astype(F32), U32)
    return (lo >> 16) | (hi & jnp.uint32(HI_MASK))


def _unpack_lo(p):
    return lax.bitcast_convert_type(p << 16, F32)


def _unpack_hi(p):
    return lax.bitcast_convert_type(p & jnp.uint32(HI_MASK), F32)


def _adaln_kernel(c_ref, w_ref, b_ref, o_ref):
    s = _silu(c_ref[...])
    o_ref[...] = jnp.dot(s.astype(BF16), w_ref[...].astype(BF16), preferred_element_type=F32) + b_ref[...]


def _adaln(cvecs, w, b):
    r, d = cvecs.shape
    n = w.shape[1]
    tn = _tile(n, 1024, LANES)
    return pl.pallas_call(
        _adaln_kernel,
        grid=(n // tn,),
        in_specs=[pl.BlockSpec((r, d), lambda j: (0, 0)),
                  pl.BlockSpec((d, tn), lambda j: (0, j)),
                  pl.BlockSpec((1, tn), lambda j: (0, j))],
        out_specs=pl.BlockSpec((r, tn), lambda j: (0, j)),
        out_shape=jax.ShapeDtypeStruct((r, n), F32),
        compiler_params=_params("parallel", vmem=VMEM_LIMIT_BYTES),
        name="adaln",
    )(cvecs, w, b)


def _pre_kernel(x_ref, pos_ref, g_ref, b_ref, sh_ref, sc_ref, o_ref):
    h = _ln(x_ref[0] + pos_ref[...], g_ref[...], b_ref[...])
    o_ref[0] = (h * (1.0 + sc_ref[0]) + sh_ref[0]).astype(o_ref.dtype)


def _pre_nopos_kernel(x_ref, g_ref, b_ref, sh_ref, sc_ref, o_ref):
    h = _ln(x_ref[0], g_ref[...], b_ref[...])
    o_ref[0] = (h * (1.0 + sc_ref[0]) + sh_ref[0]).astype(o_ref.dtype)


def _pre(x, pos, g, b, shift, scale):
    bsz, seq, d = x.shape
    tr = _tile(seq, 512, BF16_ROWS)
    per_batch = shift.shape[0] > 1
    mod_spec = pl.BlockSpec((1, 1, d), (lambda bi, i: (bi, 0, 0)) if per_batch else (lambda bi, i: (0, 0, 0)))
    vec_spec = pl.BlockSpec((1, d), lambda bi, i: (0, 0))
    x_spec = pl.BlockSpec((1, tr, d), lambda bi, i: (bi, i, 0))
    if pos is None:
        kern, ins, specs = _pre_nopos_kernel, (x, g, b, shift, scale), [x_spec, vec_spec, vec_spec, mod_spec, mod_spec]
    else:
        kern, ins = _pre_kernel, (x, pos, g, b, shift, scale)
        specs = [x_spec, pl.BlockSpec((tr, d), lambda bi, i: (i, 0)), vec_spec, vec_spec, mod_spec, mod_spec]
    return pl.pallas_call(
        kern,
        grid=(bsz, seq // tr),
        in_specs=specs,
        out_specs=pl.BlockSpec((1, tr, d), lambda bi, i: (bi, i, 0)),
        out_shape=jax.ShapeDtypeStruct((bsz, seq, d), BF16),
        compiler_params=_params("parallel", "parallel", vmem=VMEM_LIMIT_BYTES),
        name="ln_modulate",
    )(*ins)


def _mm_kernel(x_ref, w_ref, o_ref):
    o_ref[...] = jnp.dot(x_ref[...], w_ref[...], preferred_element_type=F32).astype(o_ref.dtype)


def _matmul(x, w, out_dtype, name):
    m, k = x.shape
    n = w.shape[1]
    tm = _tile(m, 1024, BF16_ROWS)
    tn = _tile(n, 512, LANES)
    return pl.pallas_call(
        _mm_kernel,
        grid=(m // tm, n // tn),
        in_specs=[pl.BlockSpec((tm, k), lambda i, j: (i, 0)),
                  pl.BlockSpec((k, tn), lambda i, j: (0, j))],
        out_specs=pl.BlockSpec((tm, tn), lambda i, j: (i, j)),
        out_shape=jax.ShapeDtypeStruct((m, n), out_dtype),
        compiler_params=_params("parallel", "arbitrary", vmem=VMEM_LIMIT_BYTES),
        name=name,
    )(x, w)


def _conv_kernel(prev_ref, cur_ref, next_ref, w_ref, b_ref, o_ref, *, tr):
    i = pl.program_id(1)
    last = pl.num_programs(1) - 1
    half = CONV_W // 2
    cur = cur_ref[0].astype(F32)
    prev = jnp.where(i > 0, prev_ref[0].astype(F32), 0.0)
    nxt = jnp.where(i < last, next_ref[0].astype(F32), 0.0)
    w = w_ref[...]
    hrows = prev.shape[0]
    row = lax.broadcasted_iota(I32, prev.shape, 0)
    acc = cur * w[half:half + 1] + b_ref[...]
    for s in range(1, half + 1):
        rc = pltpu.roll(cur, s, 0)
        rp = pltpu.roll(prev, s, 0)
        head = jnp.where(row < s, rp, rc[:hrows])
        acc = acc + jnp.concatenate([head, rc[hrows:]], axis=0) * w[half - s:half - s + 1]
        rc = pltpu.roll(cur, tr - s, 0)
        rn = pltpu.roll(nxt, hrows - s, 0)
        tail = jnp.where(row >= hrows - s, rn, rc[tr - hrows:])
        acc = acc + jnp.concatenate([rc[:tr - hrows], tail], axis=0) * w[half + s:half + s + 1]
    o_ref[0] = _silu(acc).astype(o_ref.dtype)


def _conv_silu(u, w, b):
    bsz, seq, c = u.shape
    tr = _tile(seq, 256, 2 * BF16_ROWS)
    tc = _tile(c, 512, LANES)
    hb = BF16_ROWS
    per = tr // hb
    nblk = seq // hb
    return pl.pallas_call(
        functools.partial(_conv_kernel, tr=tr),
        grid=(bsz, seq // tr, c // tc),
        in_specs=[pl.BlockSpec((1, hb, tc), lambda bi, i, j: (bi, jnp.maximum(i * per - 1, 0), j)),
                  pl.BlockSpec((1, tr, tc), lambda bi, i, j: (bi, i, j)),
                  pl.BlockSpec((1, hb, tc), lambda bi, i, j: (bi, jnp.minimum((i + 1) * per, nblk - 1), j)),
                  pl.BlockSpec((CONV_W, tc), lambda bi, i, j: (0, j)),
                  pl.BlockSpec((1, tc), lambda bi, i, j: (0, j))],
        out_specs=pl.BlockSpec((1, tr, tc), lambda bi, i, j: (bi, i, j)),
        out_shape=jax.ShapeDtypeStruct((bsz, seq, c), BF16),
        compiler_params=_params("parallel", "parallel", "parallel", vmem=VMEM_LIMIT_BYTES),
        name="dwconv_silu",
    )(u, u, u, w, b)


def _ssd_kernel(*refs, rev, mode, heads, groups, q):
    if mode == "state":
        x_ref, b_ref, c_ref, dt_ref, init_ref, alog_ref, dtb_ref, fin_ref, st_ref = refs
    elif mode == "partial":
        x_ref, b_ref, c_ref, dt_ref, init_ref, alog_ref, dtb_ref, y_ref, fin_ref, st_ref = refs
    else:
        (x_ref, b_ref, c_ref, dt_ref, init_ref, alog_ref, dtb_ref, z_ref, yo_ref, dsk_ref, nw_ref,
         y_ref, fin_ref, st_ref, yacc_ref) = refs
    p = SSD_HEAD_DIM
    n = SSD_STATE
    pair_w = 2 * p
    per_group = heads // groups
    c_idx = pl.program_id(1)

    @pl.when(c_idx == 0)
    def _():
        st_ref[...] = init_ref[0]

    dt = _softplus(dt_ref[0] + dtb_ref[...])
    a = dt * (-jnp.exp(alog_ref[...]))
    li = lax.broadcasted_iota(I32, (q, q), 0)
    si = lax.broadcasted_iota(I32, (q, q), 1)
    keep = (si >= li) if rev else (si <= li)
    tri = keep.astype(BF16)
    cum = sum(jnp.dot(tri, t, preferred_element_type=F32) for t in _split3(a))
    cum_t = cum.T
    end = cum[0:1] if rev else cum[q - 1:q]
    chunk_decay = jnp.exp(end)
    decay_to_end = jnp.exp(end - cum)
    decay_from_start = jnp.exp(cum)
    lane = lax.broadcasted_iota(I32, (q, pair_w), 1)
    lane1 = lax.broadcasted_iota(I32, (1, pair_w), 1)
    nt_dims = (((1,), (1,)), ((), ()))
    tn_dims = (((0,), (0,)), ((), ()))

    def halves(v, ha, hb, lane_idx):
        return jnp.where(lane_idx < p, v[:, ha:ha + 1], v[:, hb:hb + 1])

    scores = {}
    for j in range(heads // 2):
        ha, hb = 2 * j, 2 * j + 1
        g = ha // per_group
        bg = b_ref[0][:, g * n:(g + 1) * n]
        cols = slice(j * pair_w, (j + 1) * pair_w)
        xp = x_ref[0][:, cols].astype(F32)
        xdt = xp * halves(dt, ha, hb, lane)
        if mode != "state":
            cg = c_ref[0][:, g * n:(g + 1) * n]
            if g not in scores:
                scores[g] = lax.dot_general(cg, bg, nt_dims, preferred_element_type=F32)
            cgf = cg.astype(F32)
            rhs = jnp.concatenate([xdt.astype(BF16), st_ref[j].astype(BF16)], axis=0)
            ys = []
            for h in (ha, hb):
                diff = cum[:, h:h + 1] - cum_t[h:h + 1, :]
                decay = jnp.exp(jnp.where(keep, diff, -1e30))
                m = (scores[g] * decay).astype(BF16)
                coff = (cgf * decay_from_start[:, h:h + 1]).astype(BF16)
                ys.append(jnp.dot(jnp.concatenate([m, coff], axis=1), rhs, preferred_element_type=F32))
            y_pair = jnp.where(lane < p, ys[0], ys[1])
            if mode == "partial":
                y_ref[0, :, cols] = y_pair.astype(y_ref.dtype)
            else:
                yacc_ref[:, cols] = y_pair + yo_ref[0][:, cols].astype(F32) + dsk_ref[:, cols] * xp
        xd = (xdt * halves(decay_to_end, ha, hb, lane)).astype(BF16)
        st_ref[j] = (st_ref[j] * halves(chunk_decay, ha, hb, lane1)
                     + lax.dot_general(bg, xd, tn_dims, preferred_element_type=F32))

    if mode == "final":
        z = z_ref[0].astype(F32)
        yg = yacc_ref[...] * _silu(z)
        ms = jnp.mean(yg * yg, axis=-1, keepdims=True)
        y_ref[0] = (yg * lax.rsqrt(ms + LN_EPS) * nw_ref[...]).astype(y_ref.dtype)

    @pl.when(c_idx == pl.num_programs(1) - 1)
    def _():
        fin_ref[0] = st_ref[...]


def _ssd_scan(xbc, dtp, init, a_log_row, dt_bias_row, *, rev, mode, d_inner, z=None, y_other=None,
              d_skip_row=None, norm_w_row=None):
    bsz, seq, _ = xbc.shape
    q = SSD_CHUNK
    nc = seq // q
    heads = d_inner // SSD_HEAD_DIM
    gn = SSD_GROUPS * SSD_STATE
    hp = heads // 2
    assert 2 * SSD_HEAD_DIM == LANES and heads <= LANES and d_inner % gn == 0
    cmap = (lambda c: nc - 1 - c) if rev else (lambda c: c)
    dblk = 1 if rev else 0
    row = lambda bi, c: (0, 0)
    in_specs = [pl.BlockSpec((1, q, d_inner), lambda bi, c: (bi, cmap(c), 0)),
                pl.BlockSpec((1, q, gn), lambda bi, c: (bi, cmap(c), d_inner // gn)),
                pl.BlockSpec((1, q, gn), lambda bi, c: (bi, cmap(c), d_inner // gn + 1)),
                pl.BlockSpec((1, q, LANES), lambda bi, c: (bi, cmap(c), dblk)),
                pl.BlockSpec((1, hp, SSD_STATE, LANES), lambda bi, c: (bi, 0, 0, 0)),
                pl.BlockSpec((1, LANES), row),
                pl.BlockSpec((1, LANES), row)]
    args = [xbc, xbc, xbc, dtp, init, a_log_row, dt_bias_row]
    y_spec = pl.BlockSpec((1, q, d_inner), lambda bi, c: (bi, cmap(c), 0))
    fin_spec = pl.BlockSpec((1, hp, SSD_STATE, LANES), lambda bi, c: (bi, 0, 0, 0))
    fin_shape = jax.ShapeDtypeStruct((bsz, hp, SSD_STATE, LANES), F32)
    y_shape = jax.ShapeDtypeStruct((bsz, seq, d_inner), BF16)
    scratch = [pltpu.VMEM((hp, SSD_STATE, LANES), F32)]
    if mode == "state":
        out_specs, out_shape = fin_spec, fin_shape
    else:
        out_specs, out_shape = (y_spec, fin_spec), (y_shape, fin_shape)
    if mode == "final":
        in_specs += [y_spec, y_spec, pl.BlockSpec((1, d_inner), row), pl.BlockSpec((1, d_inner), row)]
        args += [z, y_other, d_skip_row, norm_w_row]
        scratch.append(pltpu.VMEM((q, d_inner), F32))
    out = pl.pallas_call(
        functools.partial(_ssd_kernel, rev=rev, mode=mode, heads=heads, groups=SSD_GROUPS, q=q),
        grid=(bsz, nc),
        in_specs=in_specs,
        out_specs=out_specs,
        out_shape=out_shape,
        scratch_shapes=scratch,
        compiler_params=_params("parallel", "arbitrary", vmem=VMEM_LIMIT_BYTES),
        name=f"ssd_{mode}_{'bwd' if rev else 'fwd'}",
    )(*args)
    return (None, out) if mode == "state" else out


def _cmlp_kernel(u_ref, v_ref, g_ref, b_ref, ws_ref, bs_ref, o_ref, *, chunks, ng, gd):
    ch = CMLP_CHUNK
    vln = _ln(_gelu_tanh(v_ref[...].astype(F32)), g_ref[...], b_ref[...])
    ug = _gelu_tanh(u_ref[...].astype(F32))
    for ci in range(chunks):
        rows = slice(ci * ch, (ci + 1) * ch)
        for g in range(ng):
            cols = slice(g * gd, (g + 1) * gd)
            mixed = jnp.dot(ws_ref[g], vln[rows, cols].astype(BF16), preferred_element_type=F32)
            mixed = mixed + jnp.tile(bs_ref[g], (1, gd // LANES))
            o_ref[rows, cols] = (ug[rows, cols] * mixed).astype(o_ref.dtype)


def _chunk_mlp(u, v, ln_g, ln_b, ws, bs_b):
    t, w = u.shape
    ng = ws.shape[0]
    gd = w // ng
    chunks = _tile(t // CMLP_CHUNK, 2, 1)
    tr = chunks * CMLP_CHUNK
    full = lambda i: (0, 0)
    return pl.pallas_call(
        functools.partial(_cmlp_kernel, chunks=chunks, ng=ng, gd=gd),
        grid=(t // tr,),
        in_specs=[pl.BlockSpec((tr, w), lambda i: (i, 0)),
                  pl.BlockSpec((tr, w), lambda i: (i, 0)),
                  pl.BlockSpec((1, w), full),
                  pl.BlockSpec((1, w), full),
                  pl.BlockSpec((ng, CMLP_CHUNK, CMLP_CHUNK), lambda i: (0, 0, 0)),
                  pl.BlockSpec((ng, CMLP_CHUNK, LANES), lambda i: (0, 0, 0))],
        out_specs=pl.BlockSpec((tr, w), lambda i: (i, 0)),
        out_shape=jax.ShapeDtypeStruct((t, w), BF16),
        compiler_params=_params("parallel", vmem=VMEM_LIMIT_BYTES),
        name="chunk_mlp",
    )(u, v, ln_g, ln_b, ws, bs_b)


def _merge_kernel(ys_ref, yc_ref, wss_ref, wcm_ref, gs_ref, gc_ref, o_ref):
    a = jnp.dot(ys_ref[...], wss_ref[...], preferred_element_type=F32)
    b = jnp.dot(yc_ref[...], wcm_ref[...], preferred_element_type=F32)
    o_ref[...] = (jax.nn.sigmoid(gs_ref[...].astype(F32)) * a
                  + jax.nn.sigmoid(gc_ref[...].astype(F32)) * b).astype(o_ref.dtype)


def _merge(ys, yc, w_ssd, w_cm, gs, gc):
    m = ys.shape[0]
    n = w_ssd.shape[1]
    tm = _tile(m, 1024, BF16_ROWS)
    tn = _tile(n, 512, LANES)
    lhs = lambda a: pl.BlockSpec((tm, a.shape[1]), lambda i, j: (i, 0))
    rhs = lambda a: pl.BlockSpec((a.shape[0], tn), lambda i, j: (0, j))
    tile = pl.BlockSpec((tm, tn), lambda i, j: (i, j))
    return pl.pallas_call(
        _merge_kernel,
        grid=(m // tm, n // tn),
        in_specs=[lhs(ys), lhs(yc), rhs(w_ssd), rhs(w_cm), tile, tile],
        out_specs=tile,
        out_shape=jax.ShapeDtypeStruct((m, n), BF16),
        compiler_params=_params("parallel", "arbitrary", vmem=VMEM_LIMIT_BYTES),
        name="branch_merge",
    )(ys, yc, w_ssd, w_cm, gs, gc)


def _mixer_out_kernel(m_ref, wo_ref, x_ref, pos_ref, gi_ref, bi_ref, g1_ref, l1g_ref, l1b_ref,
                      sh_ref, sc_ref, h_ref, a_ref):
    mix = jnp.dot(m_ref[0], wo_ref[...], preferred_element_type=F32)
    h0 = _ln(x_ref[0] + pos_ref[...], gi_ref[...], bi_ref[...])
    h1 = _ln(DEEPNORM_ALPHA * h0 + g1_ref[0] * mix, l1g_ref[...], l1b_ref[...])
    h_ref[0] = h1
    a_ref[0] = _pack_halves(h1 * (1.0 + sc_ref[0]) + sh_ref[0])


def _mixer_out(merged, w_o, x, pos, ln_in_g, ln_in_b, g1, ln1_g, ln1_b, sh2, sc2):
    bsz, seq, d = x.shape
    tm = _tile(seq, 256, BF16_ROWS)
    vec = pl.BlockSpec((1, d), lambda bi, i: (0, 0))
    mod = pl.BlockSpec((1, 1, d), lambda bi, i: (bi, 0, 0))
    rows = pl.BlockSpec((1, tm, d), lambda bi, i: (bi, i, 0))
    return pl.pallas_call(
        _mixer_out_kernel,
        grid=(bsz, seq // tm),
        in_specs=[rows, pl.BlockSpec((d, d), lambda bi, i: (0, 0)), rows,
                  pl.BlockSpec((tm, d), lambda bi, i: (i, 0)), vec, vec, mod, vec, vec, mod, mod],
        out_specs=(rows, pl.BlockSpec((1, tm, d // 2), lambda bi, i: (bi, i, 0))),
        out_shape=(jax.ShapeDtypeStruct((bsz, seq, d), F32), jax.ShapeDtypeStruct((bsz, seq, d // 2), U32)),
        compiler_params=_params("parallel", "arbitrary", vmem=VMEM_LIMIT_BYTES),
        name="mixer_out",
    )(merged, w_o, x, pos, ln_in_g, ln_in_b, g1, ln1_g, ln1_b, sh2, sc2)


def _route_kernel(xp_ref, wr_ref, bias_ref, i_ref, g_ref, d_ref, cnt_ref, base_ref, *, tt, n_exp):
    step = pl.program_id(0)
    ng = N_EXPERT_GROUPS
    per = n_exp // ng

    @pl.when(step == 0)
    def _():
        base_ref[...] = jnp.zeros_like(base_ref)

    xp = xp_ref[...]
    dh = xp.shape[1]
    xlo = _unpack_lo(xp).astype(BF16)
    xhi = _unpack_hi(xp).astype(BF16)
    nt_dims = (((1,), (1,)), ((), ()))
    logits = jnp.zeros((n_exp, tt), F32)
    for t in range(wr_ref.shape[0]):
        w = wr_ref[t]
        logits = logits + lax.dot_general(w[:, :dh], xlo, nt_dims, preferred_element_type=F32)
        logits = logits + lax.dot_general(w[:, dh:], xhi, nt_dims, preferred_element_type=F32)
    scores = jax.nn.sigmoid(logits)
    biased = scores + bias_ref[...]
    sub = lax.broadcasted_iota(I32, (per, tt), 0)
    slabs = [biased[g * per:(g + 1) * per] for g in range(ng)]

    gscore = []
    for v in slabs:
        m1 = jnp.max(v, axis=0, keepdims=True)
        first = jnp.min(jnp.where(v == m1, sub, per), axis=0, keepdims=True)
        m2 = jnp.max(jnp.where(sub == first, -jnp.inf, v), axis=0, keepdims=True)
        gscore.append(m1 + m2)
    masked = []
    for g in range(ng):
        beaten = jnp.zeros((1, tt), I32)
        for j in range(ng):
            if j != g:
                beaten = beaten + ((gscore[j] >= gscore[g]) if j < g else (gscore[j] > gscore[g])).astype(I32)
        masked.append(jnp.where(beaten < TOPK_GROUPS, slabs[g], -jnp.inf))
    ranks = [jnp.zeros((per, tt), I32) for _ in range(ng)]
    for jg in range(ng):
        for jr in range(per):
            vj = masked[jg][jr:jr + 1]
            for g in range(ng):
                if jg < g:
                    beats = vj >= masked[g]
                elif jg > g:
                    beats = vj > masked[g]
                else:
                    beats = (vj > masked[g]) | ((vj == masked[g]) & (sub > jr))
                ranks[g] = ranks[g] + beats.astype(I32)
    sel = jnp.concatenate([r < TOP_K for r in ranks], axis=0)
    w_sel = jnp.where(sel, scores, 0.0)
    gate = (ROUTED_SCALE * w_sel) / jnp.sum(w_sel, axis=0, keepdims=True)

    sel_b = sel.astype(BF16)
    ei = lax.broadcasted_iota(I32, (n_exp, n_exp), 0)
    ej = lax.broadcasted_iota(I32, (n_exp, n_exp), 1)
    k_pos = jnp.dot((ej < ei).astype(BF16), sel_b, preferred_element_type=F32).astype(I32)
    ti = lax.broadcasted_iota(I32, (tt, tt), 0)
    tj = lax.broadcasted_iota(I32, (tt, tt), 1)
    before = jnp.dot(sel_b, (ti < tj).astype(BF16), preferred_element_type=F32)
    tok = step * tt + lax.broadcasted_iota(I32, (n_exp, tt), 1)
    rank = (base_ref[:, :1] + before).astype(I32)
    i_ref[...] = jnp.where(sel, tok * TOP_K + k_pos, -1)
    g_ref[...] = gate
    d_ref[...] = tok - rank
    base_ref[...] = base_ref[...] + jnp.sum(sel.astype(F32), axis=1, keepdims=True)

    @pl.when(step == pl.num_programs(0) - 1)
    def _():
        cnt_ref[...] = base_ref[...].astype(I32)


def _route(a2p, wr3, bias_col):
    t, dh = a2p.shape
    n_exp = wr3.shape[1]
    assert n_exp // N_EXPERT_GROUPS == SUBLANES
    tt = _tile(t, 512, LANES)
    col = pl.BlockSpec((n_exp, tt), lambda i: (0, i))
    return pl.pallas_call(
        functools.partial(_route_kernel, tt=tt, n_exp=n_exp),
        grid=(t // tt,),
        in_specs=[pl.BlockSpec((tt, dh), lambda i: (i, 0)),
                  pl.BlockSpec(wr3.shape, lambda i: (0, 0, 0)),
                  pl.BlockSpec((n_exp, 1), lambda i: (0, 0))],
        out_specs=(col, col, col, pl.BlockSpec((n_exp, LANES), lambda i: (0, 0))),
        out_shape=(jax.ShapeDtypeStruct((n_exp, t), I32), jax.ShapeDtypeStruct((n_exp, t), F32),
                   jax.ShapeDtypeStruct((n_exp, t), I32), jax.ShapeDtypeStruct((n_exp, LANES), I32)),
        scratch_shapes=[pltpu.VMEM((n_exp, LANES), F32)],
        compiler_params=_params("arbitrary", vmem=VMEM_LIMIT_BYTES),
        name="moe_route",
    )(a2p, wr3, bias_col)


def _compact_kernel(i_ref, g_ref, d_ref, io_ref, go_ref, ds_ref, *, t, w):
    io_ref[...] = i_ref[...]
    go_ref[...] = g_ref[...]
    ds_ref[...] = d_ref[...]
    nb = t // w
    for s in range((t - 1).bit_length()):
        sh = 1 << s

        def body(c, carry, s=s, sh=sh):
            c0 = pl.multiple_of(c * w, w)
            cur = pl.ds(c0, w)
            icur, gcur, dcur = io_ref[:, cur], go_ref[:, cur], ds_ref[:, cur]
            src = c0 + (sh if sh >= w else w)
            ok = src < t
            srcs = pl.ds(pl.multiple_of(jnp.minimum(src, t - w), w), w)
            isrc = jnp.where(ok, io_ref[:, srcs], -1)
            gsrc, dsrc = go_ref[:, srcs], ds_ref[:, srcs]
            if sh >= w:
                ish, gsh, dsh = isrc, gsrc, dsrc
            else:
                ish = jnp.concatenate([icur, isrc], axis=1)[:, sh:sh + w]
                gsh = jnp.concatenate([gcur, gsrc], axis=1)[:, sh:sh + w]
                dsh = jnp.concatenate([dcur, dsrc], axis=1)[:, sh:sh + w]
            take = (ish >= 0) & (((dsh >> s) & 1) == 1)
            stay = (icur >= 0) & (((dcur >> s) & 1) == 0)
            io_ref[:, cur] = jnp.where(take, ish, jnp.where(stay, icur, -1))
            go_ref[:, cur] = jnp.where(take, gsh, jnp.where(stay, gcur, 0.0))
            ds_ref[:, cur] = jnp.where(take, dsh, dcur)
            return carry

        lax.fori_loop(0, nb, body, 0)


def _compact(idx, gate, shift):
    n_exp, t = idx.shape
    w = _tile(t, 2048, LANES)
    blk = pl.BlockSpec((SUBLANES, t), lambda i: (i, 0))
    return pl.pallas_call(
        functools.partial(_compact_kernel, t=t, w=w),
        grid=(n_exp // SUBLANES,),
        in_specs=[blk, blk, blk],
        out_specs=(blk, blk),
        out_shape=(jax.ShapeDtypeStruct((n_exp, t), I32), jax.ShapeDtypeStruct((n_exp, t), F32)),
        scratch_shapes=[pltpu.VMEM((SUBLANES, t), I32)],
        compiler_params=_params("parallel", vmem=VMEM_LIMIT_BYTES),
        name="moe_compact",
    )(idx, gate, shift)


def _moe_kernel(te_ref, tu_ref, nt_ref, idx_ref, idxn_ref, gate_ref, a2_hbm, wg_ref, wu_ref, wd_ref,
                y_hbm, xbuf, ybuf, sem_in, sem_out, *, tm, n_pairs):
    del te_ref, tu_ref
    i = pl.program_id(0)
    nt = nt_ref[0]

    def row_in(tok, slot, r):
        return pltpu.make_async_copy(a2_hbm.at[pl.ds(tok, 1)], xbuf.at[slot, pl.ds(r, 1)], sem_in.at[slot])

    def row_out(dst, slot, r):
        return pltpu.make_async_copy(ybuf.at[slot, pl.ds(r, 1)], y_hbm.at[pl.ds(dst, 1)], sem_out.at[slot])

    def start_gather(ids, slot):
        for r in range(tm):
            pair = ids[0, 0, r]
            row_in(jnp.where(pair < 0, 0, lax.shift_right_logical(pair, 3)), slot, r).start()

    def wait_gather(slot):
        for r in range(tm):
            row_in(0, slot, r).wait()

    def wait_scatter(slot):
        for r in range(tm):
            row_out(0, slot, r).wait()

    slot = lax.rem(i, 2)

    @pl.when(i == 0)
    def _():
        start_gather(idx_ref, 0)

    @pl.when(i < nt)
    def _():
        start_gather(idxn_ref, 1 - slot)
        wait_gather(slot)

        @pl.when(i >= 2)
        def _():
            wait_scatter(slot)

        xp = xbuf[slot]
        dh = xp.shape[1]
        xlo = _unpack_lo(xp).astype(BF16)
        xhi = _unpack_hi(xp).astype(BF16)
        wg = wg_ref[0]
        wu = wu_ref[0]
        g = (jnp.dot(xlo, wg[:dh], preferred_element_type=F32)
             + jnp.dot(xhi, wg[dh:], preferred_element_type=F32))
        u = (jnp.dot(xlo, wu[:dh], preferred_element_type=F32)
             + jnp.dot(xhi, wu[dh:], preferred_element_type=F32))
        gate_rows = jnp.broadcast_to(gate_ref[0], (LANES, tm)).T
        hid = (_silu(g) * u) * jnp.tile(gate_rows, (1, g.shape[1] // LANES))
        y = jnp.dot(hid.astype(BF16), wd_ref[0], preferred_element_type=F32)
        ybuf[slot] = _pack_halves(y)
        for r in range(tm):
            pair = idx_ref[0, 0, r]
            row_out(jnp.where(pair < 0, n_pairs + slot * tm + r, pair), slot, r).start()

    @pl.when(i == nt)
    def _():
        wait_gather(slot)
        wait_scatter(1 - slot)

        @pl.when(nt >= 2)
        def _():
            wait_scatter(slot)


def _moe_experts(a2p, idx_c, gate_c, te, tu, nt, w_gate, w_up, w_down, n_tiles):
    t, dh = a2p.shape
    n_exp, d, ed = w_gate.shape
    tm = MOE_TILE_ROWS
    n_pairs = t * TOP_K
    idx3 = idx_c.reshape(n_exp, 1, t)
    gate3 = gate_c.reshape(n_exp, 1, t)
    last = n_tiles - 1
    cur = lambda i, te, tu, nt: (te[i], 0, tu[i])
    nxt = lambda i, te, tu, nt: (te[jnp.minimum(i + 1, last)], 0, tu[jnp.minimum(i + 1, last)])
    wsel = lambda i, te, tu, nt: (te[i], 0, 0)
    grid_spec = pltpu.PrefetchScalarGridSpec(
        num_scalar_prefetch=3,
        grid=(n_tiles,),
        in_specs=[pl.BlockSpec((1, 1, tm), cur, memory_space=pltpu.SMEM),
                  pl.BlockSpec((1, 1, tm), nxt, memory_space=pltpu.SMEM),
                  pl.BlockSpec((1, 1, tm), cur),
                  pl.BlockSpec(memory_space=pl.ANY),
                  pl.BlockSpec((1, d, ed), wsel),
                  pl.BlockSpec((1, d, ed), wsel),
                  pl.BlockSpec((1, ed, d), wsel)],
        out_specs=pl.BlockSpec(memory_space=pl.ANY),
        scratch_shapes=[pltpu.VMEM((2, tm, dh), U32), pltpu.VMEM((2, tm, dh), U32),
                        pltpu.SemaphoreType.DMA((2,)), pltpu.SemaphoreType.DMA((2,))],
    )
    return pl.pallas_call(
        functools.partial(_moe_kernel, tm=tm, n_pairs=n_pairs),
        grid_spec=grid_spec,
        out_shape=jax.ShapeDtypeStruct((n_pairs + 2 * tm, dh), U32),
        compiler_params=_params("arbitrary", vmem=VMEM_LIMIT_BYTES),
        name="moe_experts",
    )(te, tu, nt, idx3, idx3, gate3, a2p, w_gate, w_up, w_down)


def _final_kernel(a_ref, y_ref, h_ref, wg_ref, wu_ref, wd_ref, g2_ref, lg_ref, lb_ref, o_ref):
    xp = a_ref[0]
    dh = xp.shape[1]
    xlo = _unpack_lo(xp).astype(BF16)
    xhi = _unpack_hi(xp).astype(BF16)
    g = (jnp.dot(xlo, wg_ref[:dh], preferred_element_type=F32)
         + jnp.dot(xhi, wg_ref[dh:], preferred_element_type=F32))
    u = (jnp.dot(xlo, wu_ref[:dh], preferred_element_type=F32)
         + jnp.dot(xhi, wu_ref[dh:], preferred_element_type=F32))
    shared = jnp.dot((_silu(g) * u).astype(BF16), wd_ref[...], preferred_element_type=F32)
    yp = y_ref[...]
    lo = _unpack_lo(yp[:, :dh])
    hi = _unpack_hi(yp[:, :dh])
    for k in range(1, TOP_K):
        lo = lo + _unpack_lo(yp[:, k * dh:(k + 1) * dh])
        hi = hi + _unpack_hi(yp[:, k * dh:(k + 1) * dh])
    f = jnp.concatenate([lo, hi], axis=1) + shared
    o_ref[0] = _ln(DEEPNORM_ALPHA * h_ref[0] + g2_ref[0] * f, lg_ref[...], lb_ref[...])


def _final(a2p, ybuf, h1, w_sg, w_su, w_sd, g2, ln2_g, ln2_b):
    bsz, seq, d = h1.shape
    dh = d // 2
    sd = w_sg.shape[1]
    tm = _tile(seq, 256, SUBLANES)
    per_batch = seq // tm
    vec = pl.BlockSpec((1, d), lambda bi, i: (0, 0))
    return pl.pallas_call(
        _final_kernel,
        grid=(bsz, per_batch),
        in_specs=[pl.BlockSpec((1, tm, dh), lambda bi, i: (bi, i, 0)),
                  pl.BlockSpec((tm, TOP_K * dh), lambda bi, i: (bi * per_batch + i, 0)),
                  pl.BlockSpec((1, tm, d), lambda bi, i: (bi, i, 0)),
                  pl.BlockSpec((d, sd), lambda bi, i: (0, 0)),
                  pl.BlockSpec((d, sd), lambda bi, i: (0, 0)),
                  pl.BlockSpec((sd, d), lambda bi, i: (0, 0)),
                  pl.BlockSpec((1, 1, d), lambda bi, i: (bi, 0, 0)), vec, vec],
        out_specs=pl.BlockSpec((1, tm, d), lambda bi, i: (bi, i, 0)),
        out_shape=jax.ShapeDtypeStruct((bsz, seq, d), F32),
        compiler_params=_params("parallel", "arbitrary", vmem=VMEM_LIMIT_BYTES),
        name="moe_combine_ln",
    )(a2p, ybuf, h1, w_sg, w_su, w_sd, g2, ln2_g, ln2_b)


def _sincos_2d(rows, dim):
    quarter = dim // 4
    omega = 1.0 / POS_BASE ** (jnp.arange(quarter, dtype=F32) / quarter)
    r = jnp.broadcast_to(jnp.arange(rows, dtype=F32)[:, None], (rows, GRID_W)).reshape(-1)
    col = jnp.broadcast_to(jnp.arange(GRID_W, dtype=F32)[None, :], (rows, GRID_W)).reshape(-1)
    ar = r[:, None] * omega
    ac = col[:, None] * omega
    return jnp.concatenate([jnp.sin(ar), jnp.cos(ar), jnp.sin(ac), jnp.cos(ac)], -1)


def _pad_lanes(v):
    return jnp.pad(v, (0, LANES - v.shape[0])).reshape(1, LANES)


def _tile_schedule(counts, tm, n_tiles):
    per = (counts + tm - 1) // tm
    ends = jnp.cumsum(per)
    nt = ends[-1]
    step = jnp.minimum(jnp.arange(n_tiles, dtype=I32), nt - 1)
    te = jnp.minimum(jnp.searchsorted(ends, step, side="right").astype(I32), counts.shape[0] - 1)
    tu = step - (ends[te] - per[te])
    return te.astype(I32), tu.astype(I32), nt.reshape(1).astype(I32)


def kernel(x, c, ctx, c_ctx, ln_in_g, ln_in_b, w_ada, b_ada, w_in, conv_w, conv_b, dt_bias, a_log, d_skip, ssd_norm_w, w_ssd_br, cmlp_ln_g, cmlp_ln_b, cmlp_ws, cmlp_bs, w_cmlp_br, w_o, ln1_g, ln1_b, w_router, router_bias, w_e_gate, w_e_up, w_e_down, w_sh_gate, w_sh_up, w_sh_down, ln2_g, ln2_b):
    assert w_in.shape[0] == DEPTH == 1
    bsz, seq, d = x.shape
    t = bsz * seq
    d_inner = ssd_norm_w.shape[1]
    heads = d_inner // SSD_HEAD_DIM
    gn = SSD_GROUPS * SSD_STATE
    xbc_dim = d_inner + 2 * gn
    ssd_cols = xbc_dim + 2 * heads
    cw = cmlp_ln_g.shape[1]
    row = lambda v: v.reshape(1, -1)

    w_in0 = w_in[0]
    o = ssd_cols
    w_xbc = w_in0[:, :xbc_dim].astype(BF16)
    w_dt = jnp.zeros((d, 2 * LANES), F32)
    w_dt = w_dt.at[:, :heads].set(w_in0[:, xbc_dim:xbc_dim + heads])
    w_dt = w_dt.at[:, LANES:LANES + heads].set(w_in0[:, xbc_dim + heads:ssd_cols]).astype(BF16)
    w_z = w_in0[:, o:o + d_inner].astype(BF16)
    o += d_inner
    w_u = w_in0[:, o:o + cw].astype(BF16)
    o += cw
    w_v = w_in0[:, o:o + cw].astype(BF16)
    o += cw
    w_gs = w_in0[:, o:o + d].astype(BF16)
    w_gc = w_in0[:, o + d:o + 2 * d].astype(BF16)
    a_rows = [_pad_lanes(a_log[0, k]) for k in range(2)]
    dtb_rows = [_pad_lanes(dt_bias[0, k]) for k in range(2)]
    d_skip_row = jnp.repeat(d_skip[0], SSD_HEAD_DIM).reshape(1, d_inner)
    bs_b = jnp.broadcast_to(cmlp_bs[0][:, :, None], cmlp_bs.shape[1:] + (LANES,))
    wr3 = jnp.stack(_split3(w_router[0].T))
    pos = _sincos_2d(seq // GRID_W, d).astype(x.dtype)

    n_rows = -(-(bsz + 1) // BF16_ROWS) * BF16_ROWS
    cvecs = jnp.zeros((n_rows, d), F32).at[:bsz].set(c).at[bsz].set(c_ctx)
    ada = _adaln(cvecs, w_ada[0], row(b_ada[0]))
    lat = lambda k: ada[:bsz, k * d:(k + 1) * d].reshape(bsz, 1, d)
    sh1, sc1, g1, sh2, sc2, g2 = (lat(k) for k in range(6))
    csh1 = ada[bsz:bsz + 1, :d].reshape(1, 1, d)
    csc1 = ada[bsz:bsz + 1, d:2 * d].reshape(1, 1, d)

    lg, lb = row(ln_in_g), row(ln_in_b)
    clen = ctx.shape[1]
    a_ctx = _pre(ctx, None, lg, lb, csh1, csc1).reshape(bsz * clen, d)
    xbc_c = _conv_silu(_matmul(a_ctx, w_xbc, BF16, "ctx_proj_xbc").reshape(bsz, clen, xbc_dim),
                       conv_w[0], row(conv_b[0]))
    dt_c = _matmul(a_ctx, w_dt, F32, "ctx_proj_dt").reshape(bsz, clen, 2 * LANES)
    zero_state = jnp.zeros((bsz, heads // 2, SSD_STATE, LANES), F32)
    _, s_f = _ssd_scan(xbc_c, dt_c, zero_state, a_rows[0], dtb_rows[0], rev=False, mode="state", d_inner=d_inner)
    _, s_b = _ssd_scan(xbc_c, dt_c, zero_state, a_rows[1], dtb_rows[1], rev=True, mode="state", d_inner=d_inner)

    a_lat = _pre(x, pos, lg, lb, sh1, sc1).reshape(t, d)
    xbc = _conv_silu(_matmul(a_lat, w_xbc, BF16, "proj_xbc").reshape(bsz, seq, xbc_dim), conv_w[0], row(conv_b[0]))
    dt_l = _matmul(a_lat, w_dt, F32, "proj_dt").reshape(bsz, seq, 2 * LANES)
    z = _matmul(a_lat, w_z, BF16, "proj_z").reshape(bsz, seq, d_inner)
    u = _matmul(a_lat, w_u, BF16, "proj_u")
    v = _matmul(a_lat, w_v, BF16, "proj_v")
    gs = _matmul(a_lat, w_gs, BF16, "proj_gate_ssd")
    gc = _matmul(a_lat, w_gc, BF16, "proj_gate_cmlp")
    y_b, _ = _ssd_scan(xbc, dt_l, s_b, a_rows[1], dtb_rows[1], rev=True, mode="partial", d_inner=d_inner)
    y_ssd, _ = _ssd_scan(xbc, dt_l, s_f, a_rows[0], dtb_rows[0], rev=False, mode="final", d_inner=d_inner,
                         z=z, y_other=y_b, d_skip_row=d_skip_row, norm_w_row=row(ssd_norm_w[0]))
    y_cm = _chunk_mlp(u, v, row(cmlp_ln_g[0]), row(cmlp_ln_b[0]), cmlp_ws[0].astype(BF16), bs_b)
    merged = _merge(y_ssd.reshape(t, d_inner), y_cm, w_ssd_br[0].astype(BF16), w_cmlp_br[0].astype(BF16), gs, gc)
    h1, a2p = _mixer_out(merged.reshape(bsz, seq, d), w_o[0].astype(BF16), x, pos, lg, lb, g1,
                         row(ln1_g[0]), row(ln1_b[0]), sh2, sc2)

    a2f = a2p.reshape(t, d // 2)
    idx, gate, shift, counts = _route(a2f, wr3, router_bias[0].reshape(-1, 1))
    idx_c, gate_c = _compact(idx, gate, shift)
    n_exp = w_router.shape[2]
    n_tiles = t * TOP_K // MOE_TILE_ROWS + n_exp + 1
    te, tu, nt = _tile_schedule(counts[:, 0], MOE_TILE_ROWS, n_tiles)
    ybuf = _moe_experts(a2f, idx_c, gate_c, te, tu, nt, w_e_gate[0].astype(BF16), w_e_up[0].astype(BF16),
                        w_e_down[0].astype(BF16), n_tiles)
    return _final(a2p, ybuf.reshape(-1, TOP_K * (d // 2)), h1, w_sh_gate[0].astype(BF16), w_sh_up[0].astype(BF16), w_sh_down[0].astype(BF16),
                  g2, row(ln2_g[0]), row(ln2_b[0]))
```
